```python
import jax, jax.numpy as jnp
from jax import lax
import numpy as np

D_MODEL = 1024
BATCH = 8
SEQ = 4096
DEPTH = 1
DEC_BATCH = 8
DEC_SEQ = 2048
PAST_LEN = 128

N_HEADS = 8
QK_NOPE = 64
QK_ROPE = 32
V_HEAD = 64
Q_LORA = 384
KV_LORA = 256
ATTN_WIDTH = N_HEADS * V_HEAD
ROPE_THETA = 10000.0
Q_BLOCK = 128
ATTN_SCALE = (QK_NOPE + QK_ROPE) ** -0.5
CONV_CH = D_MODEL - ATTN_WIDTH
CONV_K = 31
CONV_PAD = (CONV_K - 1) // 2
IN_WIDTH = Q_LORA + KV_LORA + QK_ROPE + 2 * CONV_CH
MIX_WIDTH = ATTN_WIDTH + CONV_CH
D_FF = -(-8 * D_MODEL // (3 * 256)) * 256
EPS = 1e-6

kernel_name = "hymba_mla_conformer_encoder"


def rms_norm(x, g):
    xf = x.astype(jnp.float32)
    y = xf * lax.rsqrt(jnp.mean(xf * xf, axis=-1, keepdims=True) + EPS)
    return (y * g.astype(jnp.float32)).astype(x.dtype)


def layer_norm(x, g, b):
    xf = x.astype(jnp.float32)
    mu = jnp.mean(xf, axis=-1, keepdims=True)
    var = jnp.mean(jnp.square(xf - mu), axis=-1, keepdims=True)
    y = (xf - mu) * lax.rsqrt(var + EPS)
    return (y * g.astype(jnp.float32) + b.astype(jnp.float32)).astype(x.dtype)


def rope_tables(seq_len):
    inv = 1.0 / (ROPE_THETA ** (jnp.arange(0, QK_ROPE, 2, dtype=jnp.float32) / QK_ROPE))
    ang = jnp.arange(seq_len, dtype=jnp.float32)[:, None] * inv[None, :]
    return jnp.cos(ang)[None, :, None, :], jnp.sin(ang)[None, :, None, :]


def apply_rope(x, cos, sin):
    xf = x.astype(jnp.float32)
    x1, x2 = jnp.split(xf, 2, axis=-1)
    return jnp.concatenate([x1 * cos - x2 * sin, x2 * cos + x1 * sin], axis=-1).astype(x.dtype)


def block_attention(q, k, v):
    B, S, H, Dq = q.shape
    nb = S // Q_BLOCK
    qb = q.reshape(B, nb, Q_BLOCK, H, Dq).transpose(1, 0, 2, 3, 4)

    def one(qi):
        s = jnp.einsum('bqhd,bkhd->bhqk', qi, k).astype(jnp.float32) * ATTN_SCALE
        p = jax.nn.softmax(s, axis=-1).astype(v.dtype)
        return jnp.einsum('bhqk,bkhd->bqhd', p, v)

    o = lax.map(one, qb)
    return o.transpose(1, 0, 2, 3, 4).reshape(B, S, H * v.shape[-1])


def mixer(h, w_in, q_norm_g, w_uq, kv_norm_g, w_ukv, dw_w, dw_b,
          conv_ln_g, conv_ln_b, attn_out_g, conv_out_g, w_out):
    B, S, _ = h.shape
    proj = h @ w_in
    cut = np.cumsum([Q_LORA, KV_LORA, QK_ROPE, CONV_CH]).tolist()
    cq, ckv, kr, conv_a, conv_g = jnp.split(proj, cut, axis=-1)

    q = (rms_norm(cq, q_norm_g) @ w_uq).reshape(B, S, N_HEADS, QK_NOPE + QK_ROPE)
    q_nope, q_rope = jnp.split(q, [QK_NOPE], axis=-1)
    kv = (rms_norm(ckv, kv_norm_g) @ w_ukv).reshape(B, S, N_HEADS, QK_NOPE + V_HEAD)
    k_nope, v = jnp.split(kv, [QK_NOPE], axis=-1)
    cos, sin = rope_tables(S)
    q_rope = apply_rope(q_rope, cos, sin)
    k_rope = apply_rope(kr[:, :, None, :], cos, sin)
    qf = jnp.concatenate([q_nope, q_rope], axis=-1)
    kf = jnp.concatenate([k_nope, jnp.broadcast_to(k_rope, (B, S, N_HEADS, QK_ROPE))], axis=-1)
    attn = block_attention(qf, kf, v)

    u = conv_a * jax.nn.sigmoid(conv_g)
    u = lax.conv_general_dilated(u, dw_w[:, None, :], window_strides=(1,),
                                 padding=[(CONV_PAD, CONV_PAD)],
                                 dimension_numbers=("NWC", "WIO", "NWC"),
                                 feature_group_count=CONV_CH) + dw_b
    u = jax.nn.silu(layer_norm(u, conv_ln_g, conv_ln_b))

    merged = jnp.concatenate([rms_norm(attn, attn_out_g), rms_norm(u, conv_out_g)], axis=-1)
    return merged @ w_out


def swiglu(h, w_gate, w_up, w_down):
    return (jax.nn.silu(h @ w_gate) * (h @ w_up)) @ w_down


def trunk(x, norm1_g, w_in, q_norm_g, w_uq, kv_norm_g, w_ukv, dw_w, dw_b,
          conv_ln_g, conv_ln_b, attn_out_g, conv_out_g, w_out,
          norm2_g, w_gate, w_up, w_down, final_g):
    for l in range(DEPTH):
        x = x + mixer(rms_norm(x, norm1_g[l]), w_in[l], q_norm_g[l], w_uq[l],
                      kv_norm_g[l], w_ukv[l], dw_w[l], dw_b[l], conv_ln_g[l],
                      conv_ln_b[l], attn_out_g[l], conv_out_g[l], w_out[l])
        x = x + swiglu(rms_norm(x, norm2_g[l]), w_gate[l], w_up[l], w_down[l])
    return rms_norm(x, final_g)


def setup_inputs(seed: int = 0) -> dict:
    key = jax.random.key(seed)
    ks = jax.random.split(key, 24)
    f32 = jnp.float32

    def w(k, shape, fan_in):
        return jax.random.normal(k, shape, f32) * (fan_in ** -0.5)

    def gain(k, shape):
        return 1.0 + 0.01 * jax.random.normal(k, shape, f32)

    def bias(k, shape):
        return 0.01 * jax.random.normal(k, shape, f32)

    L = DEPTH
    return {
        "x_prompt": jax.random.normal(ks[0], (BATCH, SEQ, D_MODEL), f32),
        "x_sample": jax.random.normal(ks[1], (DEC_BATCH, DEC_SEQ, D_MODEL), f32),
        "norm1_g": gain(ks[2], (L, D_MODEL)),
        "w_in": w(ks[3], (L, D_MODEL, IN_WIDTH), D_MODEL),
        "q_norm_g": gain(ks[4], (L, Q_LORA)),
        "w_uq": w(ks[5], (L, Q_LORA, N_HEADS * (QK_NOPE + QK_ROPE)), Q_LORA),
        "kv_norm_g": gain(ks[6], (L, KV_LORA)),
        "w_ukv": w(ks[7], (L, KV_LORA, N_HEADS * (QK_NOPE + V_HEAD)), KV_LORA),
        "dw_w": w(ks[8], (L, CONV_K, CONV_CH), CONV_K),
        "dw_b": bias(ks[9], (L, CONV_CH)),
        "conv_ln_g": gain(ks[10], (L, CONV_CH)),
        "conv_ln_b": bias(ks[11], (L, CONV_CH)),
        "attn_out_g": gain(ks[12], (L, ATTN_WIDTH)),
        "conv_out_g": gain(ks[13], (L, CONV_CH)),
        "w_out": w(ks[14], (L, MIX_WIDTH, D_MODEL), MIX_WIDTH),
        "norm2_g": gain(ks[15], (L, D_MODEL)),
        "w_gate": w(ks[16], (L, D_MODEL, D_FF), D_MODEL),
        "w_up": w(ks[17], (L, D_MODEL, D_FF), D_MODEL),
        "w_down": w(ks[18], (L, D_FF, D_MODEL), D_FF),
        "final_g": gain(ks[19], (D_MODEL,)),
    }


def reference(x_prompt, x_sample, norm1_g, w_in, q_norm_g, w_uq, kv_norm_g, w_ukv,
              dw_w, dw_b, conv_ln_g, conv_ln_b, attn_out_g, conv_out_g, w_out,
              norm2_g, w_gate, w_up, w_down, final_g):
    y_prompt = trunk(x_prompt, norm1_g, w_in, q_norm_g, w_uq, kv_norm_g, w_ukv,
                     dw_w, dw_b, conv_ln_g, conv_ln_b, attn_out_g, conv_out_g,
                     w_out, norm2_g, w_gate, w_up, w_down, final_g)
    y_sample = trunk(x_sample, norm1_g, w_in, q_norm_g, w_uq, kv_norm_g, w_ukv,
                     dw_w, dw_b, conv_ln_g, conv_ln_b, attn_out_g, conv_out_g,
                     w_out, norm2_g, w_gate, w_up, w_down, final_g)
    return (y_prompt, y_sample)
```

```python
import functools
import math

import jax
import jax.numpy as jnp
from jax import lax
from jax.experimental import pallas as pl
from jax.experimental.pallas import tpu as pltpu

D_MODEL = 1024
N_HEADS = 8
QK_NOPE = 64
QK_ROPE = 32
HALF_ROPE = QK_ROPE // 2
V_HEAD = 64
Q_LORA = 384
KV_LORA = 256
ATTN_WIDTH = N_HEADS * V_HEAD
ROPE_THETA = 10000.0
ATTN_SCALE = (QK_NOPE + QK_ROPE) ** -0.5
CONV_CH = D_MODEL - ATTN_WIDTH
CONV_K = 31
CONV_PAD = (CONV_K - 1) // 2
D_FF = 2816
EPS = 1e-6

LANES = 128
HEAD_BLK = LANES
QK_WIDTH = N_HEADS * HEAD_BLK
HALO = 16
C_Q0, C_Q1 = 0, Q_LORA
C_KV0, C_KV1 = C_Q1, C_Q1 + KV_LORA
C_A0, C_A1 = C_KV1, C_KV1 + CONV_CH
C_G0, C_G1 = C_A1, C_A1 + CONV_CH
C_KR0, C_KR1 = C_G1, C_G1 + HEAD_BLK
IN_WIDTH_P = C_KR1

VMEM_LIMIT_BYTES = 56 * 1024 * 1024

F32 = jnp.float32
BF16 = jnp.bfloat16


def _rms(x, g):
    return x * lax.rsqrt(jnp.mean(x * x, axis=-1, keepdims=True) + EPS) * g


def _rope_blk(x, c, s1, s2):
    return x * c + pltpu.roll(x, HALF_ROPE, 1) * s1 + pltpu.roll(x, HEAD_BLK - HALF_ROPE, 1) * s2


def _in_kernel(x_ref, g1_ref, w_in_ref, qg_ref, w_uq_ref, kvg_ref, w_kv_ref,
               cq_ref, s1q_ref, s2q_ref, ck_ref, s1k_ref, s2k_ref,
               q_ref, k_ref, vt_ref, u_ref):
    x = x_ref[0]
    h = _rms(x, g1_ref[...]).astype(BF16)
    proj = jnp.dot(h, w_in_ref[...], preferred_element_type=F32)
    a = proj[:, C_A0:C_A1]
    g = proj[:, C_G0:C_G1]
    u_ref[0] = a * jax.nn.sigmoid(g)

    cqn = _rms(proj[:, C_Q0:C_Q1], qg_ref[...]).astype(BF16)
    q = jnp.dot(cqn, w_uq_ref[...], preferred_element_type=F32)
    ckvn = _rms(proj[:, C_KV0:C_KV1], kvg_ref[...]).astype(BF16)
    kv = jnp.dot(ckvn, w_kv_ref[...], preferred_element_type=F32)

    krr = _rope_blk(proj[:, C_KR0:C_KR1], ck_ref[...], s1k_ref[...], s2k_ref[...])
    cq, s1q, s2q = cq_ref[...], s1q_ref[...], s2q_ref[...]
    for hd in range(N_HEADS):
        lo, hi = hd * HEAD_BLK, (hd + 1) * HEAD_BLK
        q_ref[0, hd] = _rope_blk(q[:, lo:hi], cq, s1q, s2q).astype(BF16)
        k_ref[0, hd] = (kv[:, lo:hi] + krr).astype(BF16)
    v = kv[:, QK_WIDTH:]
    vt_ref[0, 0] = v.T.astype(BF16)


def _in_stage(x, p, tabs, tile):
    B, S, _ = x.shape
    nt = S // tile
    full = lambda shape: pl.BlockSpec(shape, lambda b, i: (0,) * len(shape))
    tab = pl.BlockSpec((tile, HEAD_BLK), lambda b, i: (i, 0))
    return pl.pallas_call(
        _in_kernel,
        grid=(B, nt),
        in_specs=[
            pl.BlockSpec((1, tile, D_MODEL), lambda b, i: (b, i, 0)),
            full((1, D_MODEL)),
            full((D_MODEL, IN_WIDTH_P)),
            full((1, Q_LORA)),
            full((Q_LORA, QK_WIDTH)),
            full((1, KV_LORA)),
            full((KV_LORA, QK_WIDTH + ATTN_WIDTH)),
            tab, tab, tab, tab, tab, tab,
        ],
        out_specs=[
            pl.BlockSpec((1, N_HEADS, tile, HEAD_BLK), lambda b, i: (b, 0, i, 0)),
            pl.BlockSpec((1, N_HEADS, tile, HEAD_BLK), lambda b, i: (b, 0, i, 0)),
            pl.BlockSpec((1, 1, ATTN_WIDTH, tile), lambda b, i: (b, i, 0, 0)),
            pl.BlockSpec((1, tile, CONV_CH), lambda b, i: (b, i, 0)),
        ],
        out_shape=[
            jax.ShapeDtypeStruct((B, N_HEADS, S, HEAD_BLK), BF16),
            jax.ShapeDtypeStruct((B, N_HEADS, S, HEAD_BLK), BF16),
            jax.ShapeDtypeStruct((B, nt, ATTN_WIDTH, tile), BF16),
            jax.ShapeDtypeStruct((B, S, CONV_CH), F32),
        ],
        compiler_params=pltpu.CompilerParams(
            dimension_semantics=("arbitrary", "arbitrary"),
            vmem_limit_bytes=VMEM_LIMIT_BYTES),
        name="in_stage",
    )(x, p["g1"], p["w_in"], p["qg"], p["w_uq"], p["kvg"], p["w_kv"],
      tabs["cq"], tabs["s1q"], tabs["s2q"], tabs["ck"], tabs["s1k"], tabs["s2k"])


def _attn_kernel(q_ref, k_ref, vt_ref, o_ref, *, n_chunks, kc):
    tq = q_ref.shape[2]
    for hd in range(N_HEADS):
        qh = q_ref[0, hd]

        def body(c, carry, hd=hd, qh=qh):
            m, l, acc = carry
            start = pl.multiple_of(c * kc, kc)
            kblk = k_ref[0, hd, pl.ds(start, kc), :]
            s = lax.dot_general(kblk, qh, (((1,), (1,)), ((), ())),
                                preferred_element_type=F32)
            m_new = jnp.maximum(m, jnp.max(s, axis=0, keepdims=True))
            alpha = jnp.exp2(m - m_new)
            p = jnp.exp2(s - m_new)
            l = alpha * l + jnp.sum(p, axis=0, keepdims=True)
            vblk = vt_ref[0, c, hd * V_HEAD:(hd + 1) * V_HEAD, :]
            acc = alpha * acc + jnp.dot(vblk, p.astype(BF16), preferred_element_type=F32)
            return m_new, l, acc

        init = (jnp.full((1, tq), -jnp.inf, F32), jnp.zeros((1, tq), F32),
                jnp.zeros((V_HEAD, tq), F32))
        _, l, acc = lax.fori_loop(0, n_chunks, body, init)
        o_ref[0, hd * V_HEAD:(hd + 1) * V_HEAD, :] = acc / l


def _attn_stage(q, k, vt, tq):
    B, H, S, _ = q.shape
    n_chunks, kc = vt.shape[1], vt.shape[3]
    return pl.pallas_call(
        functools.partial(_attn_kernel, n_chunks=n_chunks, kc=kc),
        grid=(B, S // tq),
        in_specs=[
            pl.BlockSpec((1, H, tq, HEAD_BLK), lambda b, i: (b, 0, i, 0)),
            pl.BlockSpec((1, H, S, HEAD_BLK), lambda b, i: (b, 0, 0, 0)),
            pl.BlockSpec((1, n_chunks, ATTN_WIDTH, kc), lambda b, i: (b, 0, 0, 0)),
        ],
        out_specs=pl.BlockSpec((1, ATTN_WIDTH, tq), lambda b, i: (b, 0, i)),
        out_shape=jax.ShapeDtypeStruct((B, ATTN_WIDTH, S), F32),
        compiler_params=pltpu.CompilerParams(
            dimension_semantics=("arbitrary", "arbitrary"),
            vmem_limit_bytes=VMEM_LIMIT_BYTES),
        name="attn_stage",
    )(q, k, vt)


def _out_kernel(x_ref, at_ref, u_ref, up_ref, un_ref, dww_ref, dwb_ref, lng_ref, lnb_ref,
                aog_ref, cog_ref, w_out_ref, g2_ref, w_gate_ref, w_up_ref, w_down_ref,
                gf_ref, y_ref, ext_ref):
    i = pl.program_id(1)
    n = pl.num_programs(1)
    tile = x_ref.shape[1]

    ext_ref[0:HALO, :] = jnp.where(i > 0, up_ref[0], 0.0)
    ext_ref[HALO:HALO + tile, :] = u_ref[0]
    ext_ref[HALO + tile:, :] = jnp.where(i < n - 1, un_ref[0], 0.0)
    conv = jnp.zeros((tile, CONV_CH), F32) + dwb_ref[...]
    off = HALO - CONV_PAD
    for kk in range(CONV_K):
        conv = conv + ext_ref[off + kk:off + kk + tile, :] * dww_ref[kk:kk + 1, :]
    mu = jnp.mean(conv, axis=-1, keepdims=True)
    cen = conv - mu
    var = jnp.mean(cen * cen, axis=-1, keepdims=True)
    ln = cen * lax.rsqrt(var + EPS) * lng_ref[...] + lnb_ref[...]
    cact = ln * jax.nn.sigmoid(ln)
    conv_n = _rms(cact, cog_ref[...]).astype(BF16)

    attn = at_ref[0].T
    attn_n = _rms(attn, aog_ref[...]).astype(BF16)
    mix = (jnp.dot(attn_n, w_out_ref[0:ATTN_WIDTH, :], preferred_element_type=F32)
           + jnp.dot(conv_n, w_out_ref[ATTN_WIDTH:, :], preferred_element_type=F32))
    x1 = x_ref[0] + mix

    h2 = _rms(x1, g2_ref[...]).astype(BF16)
    gate = jnp.dot(h2, w_gate_ref[...], preferred_element_type=F32)
    up = jnp.dot(h2, w_up_ref[...], preferred_element_type=F32)
    act = (gate * jax.nn.sigmoid(gate) * up).astype(BF16)
    x2 = x1 + jnp.dot(act, w_down_ref[...], preferred_element_type=F32)
    y_ref[0] = _rms(x2, gf_ref[...])


def _out_stage(x, attn_t, u, p, tile):
    B, S, _ = x.shape
    nt = S // tile
    hb = tile // HALO

    def const(shape):
        return pl.BlockSpec(shape, lambda b, i: (0,) * len(shape),
                            pipeline_mode=pl.Buffered(1))

    return pl.pallas_call(
        _out_kernel,
        grid=(B, nt),
        in_specs=[
            pl.BlockSpec((1, tile, D_MODEL), lambda b, i: (b, i, 0)),
            pl.BlockSpec((1, ATTN_WIDTH, tile), lambda b, i: (b, 0, i)),
            pl.BlockSpec((1, tile, CONV_CH), lambda b, i: (b, i, 0)),
            pl.BlockSpec((1, HALO, CONV_CH),
                         lambda b, i: (b, jnp.maximum(i * hb - 1, 0), 0)),
            pl.BlockSpec((1, HALO, CONV_CH),
                         lambda b, i: (b, jnp.minimum((i + 1) * hb, nt * hb - 1), 0)),
            const((CONV_K, CONV_CH)),
            const((1, CONV_CH)), const((1, CONV_CH)), const((1, CONV_CH)),
            const((1, ATTN_WIDTH)), const((1, CONV_CH)),
            const((D_MODEL, D_MODEL)),
            const((1, D_MODEL)),
            const((D_MODEL, D_FF)), const((D_MODEL, D_FF)), const((D_FF, D_MODEL)),
            const((1, D_MODEL)),
        ],
        out_specs=pl.BlockSpec((1, tile, D_MODEL), lambda b, i: (b, i, 0)),
        out_shape=jax.ShapeDtypeStruct((B, S, D_MODEL), F32),
        scratch_shapes=[pltpu.VMEM((tile + 2 * HALO, CONV_CH), F32)],
        compiler_params=pltpu.CompilerParams(
            dimension_semantics=("arbitrary", "arbitrary"),
            vmem_limit_bytes=VMEM_LIMIT_BYTES),
        name="out_stage",
    )(x, attn_t, u, u, u, p["dw_w"], p["dw_b"], p["ln_g"], p["ln_b"], p["aog"], p["cog"],
      p["w_out"], p["g2"], p["w_gate"], p["w_up"], p["w_down"], p["gf"])


def _prep_params(norm1_g, w_in, q_norm_g, w_uq, kv_norm_g, w_ukv, dw_w, dw_b, conv_ln_g,
                 conv_ln_b, attn_out_g, conv_out_g, w_out, norm2_g, w_gate, w_up, w_down,
                 final_g):
    row = lambda v: v.reshape(1, -1).astype(F32)
    w_in = w_in[0]
    o_kv = Q_LORA
    o_kr = Q_LORA + KV_LORA
    o_a = o_kr + QK_ROPE
    o_g = o_a + CONV_CH
    kr_blk = jnp.zeros((D_MODEL, HEAD_BLK), F32).at[:, QK_NOPE:QK_NOPE + QK_ROPE].set(
        w_in[:, o_kr:o_a])
    w_in_p = jnp.concatenate(
        [w_in[:, :o_kv], w_in[:, o_kv:o_kr], w_in[:, o_a:o_g], w_in[:, o_g:], kr_blk], axis=1)

    qk = QK_NOPE + QK_ROPE
    w_uq_h = w_uq[0].reshape(Q_LORA, N_HEADS, qk)
    w_uq_p = jnp.pad(w_uq_h, ((0, 0), (0, 0), (0, HEAD_BLK - qk))).reshape(Q_LORA, QK_WIDTH)
    w_ukv_h = w_ukv[0].reshape(KV_LORA, N_HEADS, QK_NOPE + V_HEAD)
    w_k = jnp.pad(w_ukv_h[:, :, :QK_NOPE],
                  ((0, 0), (0, 0), (0, HEAD_BLK - QK_NOPE))).reshape(KV_LORA, QK_WIDTH)
    w_v = w_ukv_h[:, :, QK_NOPE:].reshape(KV_LORA, ATTN_WIDTH)
    w_kv_p = jnp.concatenate([w_k, w_v], axis=1)
    return dict(
        g1=row(norm1_g[0]), w_in=w_in_p.astype(BF16), qg=row(q_norm_g[0]),
        w_uq=w_uq_p.astype(BF16), kvg=row(kv_norm_g[0]), w_kv=w_kv_p.astype(BF16),
        dw_w=dw_w[0].astype(F32), dw_b=row(dw_b[0]), ln_g=row(conv_ln_g[0]),
        ln_b=row(conv_ln_b[0]), aog=row(attn_out_g[0]), cog=row(conv_out_g[0]),
        w_out=w_out[0].astype(BF16), g2=row(norm2_g[0]), w_gate=w_gate[0].astype(BF16),
        w_up=w_up[0].astype(BF16), w_down=w_down[0].astype(BF16), gf=row(final_g))


def _rope_tables(seq_len):
    inv = 1.0 / (ROPE_THETA ** (jnp.arange(0, QK_ROPE, 2, dtype=F32) / QK_ROPE))
    ang = jnp.arange(seq_len, dtype=F32)[:, None] * inv[None, :]
    cos, sin = jnp.cos(ang), jnp.sin(ang)
    z = lambda n: jnp.zeros((seq_len, n), F32)
    one = jnp.ones((seq_len, QK_NOPE), F32)
    tail = HEAD_BLK - QK_NOPE - QK_ROPE
    c = jnp.concatenate([one, cos, cos, z(tail)], axis=1)
    s1 = jnp.concatenate([z(QK_NOPE), z(HALF_ROPE), sin, z(tail)], axis=1)
    s2 = jnp.concatenate([z(QK_NOPE), -sin, z(HALF_ROPE), z(tail)], axis=1)
    qs = ATTN_SCALE * math.log2(math.e)
    return dict(cq=c * qs, s1q=s1 * qs, s2q=s2 * qs, ck=c, s1k=s1, s2k=s2)


def _trunk(x, p, tile, tq):
    tabs = _rope_tables(x.shape[1])
    q, k, vt, u = _in_stage(x, p, tabs, tile)
    attn_t = _attn_stage(q, k, vt, tq)
    return _out_stage(x, attn_t, u, p, tile)


def kernel(x_prompt, x_sample, norm1_g, w_in, q_norm_g, w_uq, kv_norm_g, w_ukv, dw_w, dw_b,
           conv_ln_g, conv_ln_b, attn_out_g, conv_out_g, w_out, norm2_g, w_gate, w_up, w_down,
           final_g):
    p = _prep_params(norm1_g, w_in, q_norm_g, w_uq, kv_norm_g, w_ukv, dw_w, dw_b, conv_ln_g,
                     conv_ln_b, attn_out_g, conv_out_g, w_out, norm2_g, w_gate, w_up, w_down,
                     final_g)
    y_prompt = _trunk(x_prompt, p, tile=512, tq=512)
    y_sample = _trunk(x_sample, p, tile=512, tq=512)
    return (y_prompt, y_sample)
```

```python
import functools
import math

import jax
import jax.numpy as jnp
from jax import lax
from jax.experimental import pallas as pl
from jax.experimental.pallas import tpu as pltpu

D_MODEL = 1024
N_HEADS = 8
QK_NOPE = 64
QK_ROPE = 32
HALF_ROPE = QK_ROPE // 2
V_HEAD = 64
Q_LORA = 384
KV_LORA = 256
ATTN_WIDTH = N_HEADS * V_HEAD
ROPE_THETA = 10000.0
ATTN_SCALE = (QK_NOPE + QK_ROPE) ** -0.5
CONV_CH = D_MODEL - ATTN_WIDTH
CONV_K = 31
CONV_PAD = (CONV_K - 1) // 2
D_FF = 2816
EPS = 1e-6

LANES = 128
HEAD_BLK = LANES
QK_WIDTH = N_HEADS * HEAD_BLK
HALO = 16
REDUCE_SLABS = 8
C_Q0, C_Q1 = 0, Q_LORA
C_KV0, C_KV1 = C_Q1, C_Q1 + KV_LORA
C_A0, C_A1 = C_KV1, C_KV1 + CONV_CH
C_G0, C_G1 = C_A1, C_A1 + CONV_CH
C_KR0, C_KR1 = C_G1, C_G1 + HEAD_BLK
IN_WIDTH_P = C_KR1

VMEM_LIMIT_BYTES = 56 * 1024 * 1024

F32 = jnp.float32
BF16 = jnp.bfloat16


def _rms(x, g):
    return x * lax.rsqrt(jnp.mean(x * x, axis=-1, keepdims=True) + EPS) * g


def _rope_blk(x, c, s1, s2):
    return x * c + pltpu.roll(x, HALF_ROPE, 1) * s1 + pltpu.roll(x, HEAD_BLK - HALF_ROPE, 1) * s2


def _in_kernel(x_ref, g1_ref, w_in_ref, qg_ref, w_uq_ref, kvg_ref, w_kv_ref,
               cq_ref, s1q_ref, s2q_ref, ck_ref, s1k_ref, s2k_ref,
               q_ref, k_ref, vt_ref, u_ref):
    x = x_ref[0]
    h = _rms(x, g1_ref[...]).astype(BF16)
    proj = jnp.dot(h, w_in_ref[...], preferred_element_type=F32)
    a = proj[:, C_A0:C_A1]
    g = proj[:, C_G0:C_G1]
    u_ref[0] = a * jax.nn.sigmoid(g)

    cqn = _rms(proj[:, C_Q0:C_Q1], qg_ref[...]).astype(BF16)
    q = jnp.dot(cqn, w_uq_ref[...], preferred_element_type=F32)
    ckvn = _rms(proj[:, C_KV0:C_KV1], kvg_ref[...]).astype(BF16)
    kv = jnp.dot(ckvn, w_kv_ref[...], preferred_element_type=F32)

    krr = _rope_blk(proj[:, C_KR0:C_KR1], ck_ref[...], s1k_ref[...], s2k_ref[...])
    cq, s1q, s2q = cq_ref[...], s1q_ref[...], s2q_ref[...]
    for hd in range(N_HEADS):
        lo, hi = hd * HEAD_BLK, (hd + 1) * HEAD_BLK
        q_ref[0, hd] = _rope_blk(q[:, lo:hi], cq, s1q, s2q).astype(BF16)
        k_ref[0, hd] = (kv[:, lo:hi] + krr).astype(BF16)
    v = kv[:, QK_WIDTH:]
    vt_ref[0, 0] = v.T.astype(BF16)


def _in_stage(x, p, tabs, tile):
    B, S, _ = x.shape
    nt = S // tile
    full = lambda shape: pl.BlockSpec(shape, lambda b, i: (0,) * len(shape))
    tab = pl.BlockSpec((tile, HEAD_BLK), lambda b, i: (i, 0))
    return pl.pallas_call(
        _in_kernel,
        grid=(B, nt),
        in_specs=[
            pl.BlockSpec((1, tile, D_MODEL), lambda b, i: (b, i, 0)),
            full((1, D_MODEL)),
            full((D_MODEL, IN_WIDTH_P)),
            full((1, Q_LORA)),
            full((Q_LORA, QK_WIDTH)),
            full((1, KV_LORA)),
            full((KV_LORA, QK_WIDTH + ATTN_WIDTH)),
            tab, tab, tab, tab, tab, tab,
        ],
        out_specs=[
            pl.BlockSpec((1, N_HEADS, tile, HEAD_BLK), lambda b, i: (b, 0, i, 0)),
            pl.BlockSpec((1, N_HEADS, tile, HEAD_BLK), lambda b, i: (b, 0, i, 0)),
            pl.BlockSpec((1, 1, ATTN_WIDTH, tile), lambda b, i: (b, i, 0, 0)),
            pl.BlockSpec((1, tile, CONV_CH), lambda b, i: (b, i, 0)),
        ],
        out_shape=[
            jax.ShapeDtypeStruct((B, N_HEADS, S, HEAD_BLK), BF16),
            jax.ShapeDtypeStruct((B, N_HEADS, S, HEAD_BLK), BF16),
            jax.ShapeDtypeStruct((B, nt, ATTN_WIDTH, tile), BF16),
            jax.ShapeDtypeStruct((B, S, CONV_CH), F32),
        ],
        compiler_params=pltpu.CompilerParams(
            dimension_semantics=("arbitrary", "arbitrary"),
            vmem_limit_bytes=VMEM_LIMIT_BYTES),
        name="in_stage",
    )(x, p["g1"], p["w_in"], p["qg"], p["w_uq"], p["kvg"], p["w_kv"],
      tabs["cq"], tabs["s1q"], tabs["s2q"], tabs["ck"], tabs["s1k"], tabs["s2k"])


def _attn_kernel(q_ref, k_ref, vt_ref, o_ref, sa0_ref, sa1_ref, sb0_ref, sb1_ref,
                 pa0_ref, pa1_ref, pb0_ref, pb1_ref, acc_ref, *, n_chunks, kc):
    tq = q_ref.shape[2]
    n_pairs = N_HEADS * n_chunks // 2

    def head_chunk(t):
        if isinstance(t, int):
            return t // n_chunks, t % n_chunks
        return lax.div(t, n_chunks), lax.rem(t, n_chunks)

    def col_reduce(op, x):
        g = x.reshape(REDUCE_SLABS, kc // REDUCE_SLABS, tq)
        return op(op(g, axis=0), axis=0, keepdims=True)

    def scores(t, s_ref):
        hd, c = head_chunk(t)
        kblk = k_ref[0, hd, pl.ds(pl.multiple_of(c * kc, kc), kc), :]
        s_ref[...] = lax.dot_general(kblk, q_ref[0, hd], (((1,), (1,)), ((), ())),
                                     preferred_element_type=F32)

    def softmax(t, s_ref, p_ref, m, l):
        _, c = head_chunk(t)
        first = c == 0
        m = jnp.where(first, -jnp.inf, m)
        l = jnp.where(first, 0.0, l)
        s = s_ref[...]
        m_new = jnp.maximum(m, col_reduce(jnp.max, s))
        alpha = jnp.exp2(m - m_new)
        p = jnp.exp2(s - m_new)
        p_ref[...] = p.astype(BF16)
        return m_new, alpha * l + col_reduce(jnp.sum, p), alpha

    def pv_update(t, p_ref, alpha, l):
        hd, c = head_chunk(t)
        rows = pl.ds(pl.multiple_of(hd * V_HEAD, V_HEAD), V_HEAD)
        acc = jnp.where(c == 0, 0.0, acc_ref[...])
        acc = alpha * acc + jnp.dot(vt_ref[0, c, rows, :], p_ref[...],
                                    preferred_element_type=F32)
        acc_ref[...] = acc
        o_ref[0, rows, :] = acc * (1.0 / l)

    s_sets = ((sa0_ref, sa1_ref), (sb0_ref, sb1_ref))
    p_sets = ((pa0_ref, pa1_ref), (pb0_ref, pb1_ref))

    def pv_pair(u, par, st):
        _, _, a0, l0, a1, l1 = st
        pv_update(2 * u, p_sets[par][0], a0, l0)
        pv_update(2 * u + 1, p_sets[par][1], a1, l1)

    def softmax_pair(u, par, st):
        m, l = st[0], st[1]
        m, l0, a0 = softmax(2 * u, s_sets[par][0], p_sets[par][0], m, l)
        m, l1, a1 = softmax(2 * u + 1, s_sets[par][1], p_sets[par][1], m, l0)
        return m, l1, a0, l0, a1, l1

    def scores_pair(u, par):
        scores(2 * u, s_sets[par][0])
        scores(2 * u + 1, s_sets[par][1])

    def step(u, par, st):
        pv_pair(u - 2, par, st)
        st = softmax_pair(u - 1, 1 - par, st)
        scores_pair(u, par)
        return st

    acc_ref[...] = jnp.zeros_like(acc_ref)
    zero = jnp.zeros((1, tq), F32)
    st = (jnp.full((1, tq), -jnp.inf, F32), zero, zero, zero, zero, zero)
    scores_pair(0, 0)
    st = softmax_pair(0, 0, st)
    scores_pair(1, 1)

    for u in range(2, n_pairs):
        st = step(u, u % 2, st)
    pv_pair(n_pairs - 2, 0, st)
    st = softmax_pair(n_pairs - 1, 1, st)
    pv_pair(n_pairs - 1, 1, st)


def _attn_stage(q, k, vt, tq):
    B, H, S, _ = q.shape
    n_chunks, kc = vt.shape[1], vt.shape[3]
    return pl.pallas_call(
        functools.partial(_attn_kernel, n_chunks=n_chunks, kc=kc),
        grid=(B, S // tq),
        in_specs=[
            pl.BlockSpec((1, H, tq, HEAD_BLK), lambda b, i: (b, 0, i, 0)),
            pl.BlockSpec((1, H, S, HEAD_BLK), lambda b, i: (b, 0, 0, 0)),
            pl.BlockSpec((1, n_chunks, ATTN_WIDTH, kc), lambda b, i: (b, 0, 0, 0)),
        ],
        out_specs=pl.BlockSpec((1, ATTN_WIDTH, tq), lambda b, i: (b, 0, i)),
        out_shape=jax.ShapeDtypeStruct((B, ATTN_WIDTH, S), F32),
        scratch_shapes=(
            [pltpu.VMEM((kc, tq), F32)] * 4 + [pltpu.VMEM((kc, tq), BF16)] * 4
            + [pltpu.VMEM((V_HEAD, tq), F32)]),
        compiler_params=pltpu.CompilerParams(
            dimension_semantics=("arbitrary", "arbitrary"),
            vmem_limit_bytes=VMEM_LIMIT_BYTES),
        name="attn_stage",
    )(q, k, vt)


def _out_kernel(x_ref, at_ref, u_ref, up_ref, un_ref, dww_ref, dwb_ref, lng_ref, lnb_ref,
                aog_ref, cog_ref, w_out_ref, g2_ref, w_gate_ref, w_up_ref, w_down_ref,
                gf_ref, y_ref, ext_ref):
    i = pl.program_id(1)
    n = pl.num_programs(1)
    tile = x_ref.shape[1]

    ext_ref[0:HALO, :] = jnp.where(i > 0, up_ref[0], 0.0)
    ext_ref[HALO:HALO + tile, :] = u_ref[0]
    ext_ref[HALO + tile:, :] = jnp.where(i < n - 1, un_ref[0], 0.0)
    conv = jnp.zeros((tile, CONV_CH), F32) + dwb_ref[...]
    off = HALO - CONV_PAD
    for kk in range(CONV_K):
        conv = conv + ext_ref[off + kk:off + kk + tile, :] * dww_ref[kk:kk + 1, :]
    mu = jnp.mean(conv, axis=-1, keepdims=True)
    cen = conv - mu
    var = jnp.mean(cen * cen, axis=-1, keepdims=True)
    ln = cen * lax.rsqrt(var + EPS) * lng_ref[...] + lnb_ref[...]
    cact = ln * jax.nn.sigmoid(ln)
    conv_n = _rms(cact, cog_ref[...]).astype(BF16)

    attn = at_ref[0].T
    attn_n = _rms(attn, aog_ref[...]).astype(BF16)
    mix = (jnp.dot(attn_n, w_out_ref[0:ATTN_WIDTH, :], preferred_element_type=F32)
           + jnp.dot(conv_n, w_out_ref[ATTN_WIDTH:, :], preferred_element_type=F32))
    x1 = x_ref[0] + mix

    h2 = _rms(x1, g2_ref[...]).astype(BF16)
    gate = jnp.dot(h2, w_gate_ref[...], preferred_element_type=F32)
    up = jnp.dot(h2, w_up_ref[...], preferred_element_type=F32)
    act = (gate * jax.nn.sigmoid(gate) * up).astype(BF16)
    x2 = x1 + jnp.dot(act, w_down_ref[...], preferred_element_type=F32)
    y_ref[0] = _rms(x2, gf_ref[...])


def _out_stage(x, attn_t, u, p, tile):
    B, S, _ = x.shape
    nt = S // tile
    hb = tile // HALO

    def const(shape):
        return pl.BlockSpec(shape, lambda b, i: (0,) * len(shape),
                            pipeline_mode=pl.Buffered(1))

    return pl.pallas_call(
        _out_kernel,
        grid=(B, nt),
        in_specs=[
            pl.BlockSpec((1, tile, D_MODEL), lambda b, i: (b, i, 0)),
            pl.BlockSpec((1, ATTN_WIDTH, tile), lambda b, i: (b, 0, i)),
            pl.BlockSpec((1, tile, CONV_CH), lambda b, i: (b, i, 0)),
            pl.BlockSpec((1, HALO, CONV_CH),
                         lambda b, i: (b, jnp.maximum(i * hb - 1, 0), 0)),
            pl.BlockSpec((1, HALO, CONV_CH),
                         lambda b, i: (b, jnp.minimum((i + 1) * hb, nt * hb - 1), 0)),
            const((CONV_K, CONV_CH)),
            const((1, CONV_CH)), const((1, CONV_CH)), const((1, CONV_CH)),
            const((1, ATTN_WIDTH)), const((1, CONV_CH)),
            const((D_MODEL, D_MODEL)),
            const((1, D_MODEL)),
            const((D_MODEL, D_FF)), const((D_MODEL, D_FF)), const((D_FF, D_MODEL)),
            const((1, D_MODEL)),
        ],
        out_specs=pl.BlockSpec((1, tile, D_MODEL), lambda b, i: (b, i, 0)),
        out_shape=jax.ShapeDtypeStruct((B, S, D_MODEL), F32),
        scratch_shapes=[pltpu.VMEM((tile + 2 * HALO, CONV_CH), F32)],
        compiler_params=pltpu.CompilerParams(
            dimension_semantics=("arbitrary", "arbitrary"),
            vmem_limit_bytes=VMEM_LIMIT_BYTES),
        name="out_stage",
    )(x, attn_t, u, u, u, p["dw_w"], p["dw_b"], p["ln_g"], p["ln_b"], p["aog"], p["cog"],
      p["w_out"], p["g2"], p["w_gate"], p["w_up"], p["w_down"], p["gf"])


def _prep_params(norm1_g, w_in, q_norm_g, w_uq, kv_norm_g, w_ukv, dw_w, dw_b, conv_ln_g,
                 conv_ln_b, attn_out_g, conv_out_g, w_out, norm2_g, w_gate, w_up, w_down,
                 final_g):
    row = lambda v: v.reshape(1, -1).astype(F32)
    w_in = w_in[0]
    o_kv = Q_LORA
    o_kr = Q_LORA + KV_LORA
    o_a = o_kr + QK_ROPE
    o_g = o_a + CONV_CH
    kr_blk = jnp.zeros((D_MODEL, HEAD_BLK), F32).at[:, QK_NOPE:QK_NOPE + QK_ROPE].set(
        w_in[:, o_kr:o_a])
    w_in_p = jnp.concatenate(
        [w_in[:, :o_kv], w_in[:, o_kv:o_kr], w_in[:, o_a:o_g], w_in[:, o_g:], kr_blk], axis=1)

    qk = QK_NOPE + QK_ROPE
    w_uq_h = w_uq[0].reshape(Q_LORA, N_HEADS, qk)
    w_uq_p = jnp.pad(w_uq_h, ((0, 0), (0, 0), (0, HEAD_BLK - qk))).reshape(Q_LORA, QK_WIDTH)
    w_ukv_h = w_ukv[0].reshape(KV_LORA, N_HEADS, QK_NOPE + V_HEAD)
    w_k = jnp.pad(w_ukv_h[:, :, :QK_NOPE],
                  ((0, 0), (0, 0), (0, HEAD_BLK - QK_NOPE))).reshape(KV_LORA, QK_WIDTH)
    w_v = w_ukv_h[:, :, QK_NOPE:].reshape(KV_LORA, ATTN_WIDTH)
    w_kv_p = jnp.concatenate([w_k, w_v], axis=1)
    return dict(
        g1=row(norm1_g[0]), w_in=w_in_p.astype(BF16), qg=row(q_norm_g[0]),
        w_uq=w_uq_p.astype(BF16), kvg=row(kv_norm_g[0]), w_kv=w_kv_p.astype(BF16),
        dw_w=dw_w[0].astype(F32), dw_b=row(dw_b[0]), ln_g=row(conv_ln_g[0]),
        ln_b=row(conv_ln_b[0]), aog=row(attn_out_g[0]), cog=row(conv_out_g[0]),
        w_out=w_out[0].astype(BF16), g2=row(norm2_g[0]), w_gate=w_gate[0].astype(BF16),
        w_up=w_up[0].astype(BF16), w_down=w_down[0].astype(BF16), gf=row(final_g))


def _rope_tables(seq_len):
    inv = 1.0 / (ROPE_THETA ** (jnp.arange(0, QK_ROPE, 2, dtype=F32) / QK_ROPE))
    ang = jnp.arange(seq_len, dtype=F32)[:, None] * inv[None, :]
    cos, sin = jnp.cos(ang), jnp.sin(ang)
    z = lambda n: jnp.zeros((seq_len, n), F32)
    one = jnp.ones((seq_len, QK_NOPE), F32)
    tail = HEAD_BLK - QK_NOPE - QK_ROPE
    c = jnp.concatenate([one, cos, cos, z(tail)], axis=1)
    s1 = jnp.concatenate([z(QK_NOPE), z(HALF_ROPE), sin, z(tail)], axis=1)
    s2 = jnp.concatenate([z(QK_NOPE), -sin, z(HALF_ROPE), z(tail)], axis=1)
    qs = ATTN_SCALE * math.log2(math.e)
    return dict(cq=c * qs, s1q=s1 * qs, s2q=s2 * qs, ck=c, s1k=s1, s2k=s2)


def _trunk(x, p, tile, tq):
    tabs = _rope_tables(x.shape[1])
    q, k, vt, u = _in_stage(x, p, tabs, tile)
    attn_t = _attn_stage(q, k, vt, tq)
    return _out_stage(x, attn_t, u, p, tile)


def kernel(x_prompt, x_sample, norm1_g, w_in, q_norm_g, w_uq, kv_norm_g, w_ukv, dw_w, dw_b,
           conv_ln_g, conv_ln_b, attn_out_g, conv_out_g, w_out, norm2_g, w_gate, w_up, w_down,
           final_g):
    p = _prep_params(norm1_g, w_in, q_norm_g, w_uq, kv_norm_g, w_ukv, dw_w, dw_b, conv_ln_g,
                     conv_ln_b, attn_out_g, conv_out_g, w_out, norm2_g, w_gate, w_up, w_down,
                     final_g)
    y_prompt = _trunk(x_prompt, p, tile=512, tq=512)
    y_sample = _trunk(x_sample, p, tile=512, tq=512)
    return (y_prompt, y_sample)
```
